```python
import math
import jax
import jax.numpy as jnp
from jax import lax
import numpy as np

D_MODEL = 1024
BATCH = 4
SEQ = 4096
DEPTH = 1

ATT_HEADS = 4
ATT_HEAD_DIM = 64
ATT_WIDTH = ATT_HEADS * 2 * ATT_HEAD_DIM
ROPE_DIM = ATT_HEAD_DIM // 4
ROPE_THETA = 500000.0
Q_BLOCK = 128

SSM_INNER = 512
SSM_HEAD_DIM = 64
SSM_HEADS = SSM_INNER // SSM_HEAD_DIM
SSM_GROUPS = 2
SSM_HEADS_PER_GROUP = SSM_HEADS // SSM_GROUPS
SSM_STATE = 128
SSM_CONV = 3
SSM_CHUNK = 128
CONV_CH = SSM_INNER + 2 * SSM_GROUPS * SSM_STATE

MIX_WIDTH = ATT_WIDTH + SSM_INNER
IN_SPLITS = (ATT_WIDTH, 2 * ATT_WIDTH, 3 * ATT_WIDTH, 3 * ATT_WIDTH + SSM_INNER,
             3 * ATT_WIDTH + SSM_INNER + CONV_CH)
IN_COLS = 3 * ATT_WIDTH + SSM_INNER + CONV_CH + SSM_HEADS

D_FF = 2816
FFN_CONV = 3

N_MOD = 6
EPS = 1e-6

kernel_name = 'hybrid_diffattn_ssd_encoder_layer'


def rmsnorm(x, w):
    xf = x.astype(jnp.float32)
    y = xf * lax.rsqrt(jnp.mean(xf * xf, axis=-1, keepdims=True) + EPS)
    return (y * w.astype(jnp.float32)).astype(x.dtype)


def dwconv_centred(x, w, b):
    k = w.shape[0]
    pad = k // 2
    y = lax.conv_general_dilated(
        x, w.astype(x.dtype)[:, None, :], window_strides=(1,), padding=[(pad, pad)],
        dimension_numbers=('NWC', 'WIO', 'NWC'), feature_group_count=x.shape[-1])
    return y + b.astype(x.dtype)


def partial_rope(t, cos, sin):
    half = ROPE_DIM // 2
    t1 = t[..., :half]
    t2 = t[..., half:ROPE_DIM]
    return jnp.concatenate([t1 * cos - t2 * sin, t2 * cos + t1 * sin, t[..., ROPE_DIM:]], axis=-1)


def diff_attention(q, k, v, lam, lam_init, subln_w):
    b, s = q.shape[0], q.shape[1]
    scale = ATT_HEAD_DIM ** -0.5
    nb = s // Q_BLOCK
    qb = q.reshape(b, nb, Q_BLOCK, 2 * ATT_HEADS, ATT_HEAD_DIM).transpose(1, 0, 2, 3, 4)

    def block(q_blk):
        sc = jnp.einsum('bqhd,bkhd->bhqk', q_blk, k).astype(jnp.float32) * scale
        p = jax.nn.softmax(sc, axis=-1).reshape(b, ATT_HEADS, 2, Q_BLOCK, s)
        a = p[:, :, 0] - lam * p[:, :, 1]
        return jnp.einsum('bhqk,bkhe->bqhe', a.astype(v.dtype), v)

    o = lax.map(block, qb)
    o = o.transpose(1, 0, 2, 3, 4).reshape(b, s, ATT_HEADS, 2 * ATT_HEAD_DIM)
    o = rmsnorm(o, subln_w) * (1.0 - lam_init)
    return o.reshape(b, s, ATT_WIDTH)


def segsum(a):
    t = a.shape[-1]
    cs = jnp.cumsum(a, axis=-1)
    seg = cs[..., :, None] - cs[..., None, :]
    mask = jnp.tril(jnp.ones((t, t), dtype=bool))
    return jnp.where(mask, seg, -jnp.inf)


def ssd_chunked(xdt, adt, bm, cm):
    b, s, g, r, p = xdt.shape
    n = bm.shape[-1]
    nc = s // SSM_CHUNK
    x = xdt.reshape(b, nc, SSM_CHUNK, g, r, p)
    a = adt.reshape(b, nc, SSM_CHUNK, g, r).transpose(0, 3, 4, 1, 2)
    bq = bm.reshape(b, nc, SSM_CHUNK, g, n)
    cq = cm.reshape(b, nc, SSM_CHUNK, g, n)
    a_cs = jnp.cumsum(a, axis=-1)
    decay_in = jnp.exp(segsum(a))
    cb = jnp.einsum('bclgn,bcsgn->bgcls', cq, bq)
    y_diag = jnp.einsum('bgcls,bgrcls,bcsgrp->bclgrp', cb, decay_in, x)
    decay_st = jnp.exp(a_cs[..., -1:] - a_cs)
    states = jnp.einsum('bcsgn,bgrcs,bcsgrp->bcgrpn', bq, decay_st, x)
    chunk_decay = jnp.exp(a_cs[..., -1])

    def step(h, inp):
        st, dcy = inp
        return dcy[..., None, None] * h + st, h

    h0 = jnp.zeros((b, g, r, p, n), dtype=states.dtype)
    _, prev = lax.scan(step, h0, (states.transpose(1, 0, 2, 3, 4, 5),
                                  chunk_decay.transpose(3, 0, 1, 2)))
    prev = prev.transpose(1, 0, 2, 3, 4, 5)
    y_off = jnp.einsum('bclgn,bcgrpn,bgrcl->bclgrp', cq, prev, jnp.exp(a_cs))
    return (y_diag + y_off).reshape(b, s, g * r * p)


def ssd_mixer(z, xbc, dt_raw, conv_w, conv_b, dt_bias, a_log, d_skip, norm_w):
    b, s = z.shape[0], z.shape[1]
    g, r, p, n = SSM_GROUPS, SSM_HEADS_PER_GROUP, SSM_HEAD_DIM, SSM_STATE
    xbc = jax.nn.silu(dwconv_centred(xbc, conv_w, conv_b))
    xs = xbc[..., :SSM_INNER].reshape(b, s, g, r, p)
    bm = xbc[..., SSM_INNER:SSM_INNER + g * n].reshape(b, s, g, n)
    cm = xbc[..., SSM_INNER + g * n:].reshape(b, s, g, n)
    dt_f = jax.nn.softplus(dt_raw.astype(jnp.float32) + dt_bias[0].astype(jnp.float32)).reshape(b, s, g, r)
    dt_b = jax.nn.softplus(dt_raw.astype(jnp.float32) + dt_bias[1].astype(jnp.float32)).reshape(b, s, g, r)
    a_f = -jnp.exp(a_log[0].astype(jnp.float32)).reshape(g, r)
    a_b = -jnp.exp(a_log[1].astype(jnp.float32)).reshape(g, r)
    y_f = ssd_chunked(xs * dt_f[..., None], dt_f * a_f, bm, cm)
    y_b = jnp.flip(ssd_chunked(jnp.flip(xs * dt_b[..., None], 1), jnp.flip(dt_b * a_b, 1),
                               jnp.flip(bm, 1), jnp.flip(cm, 1)), 1)
    skip = xs * (d_skip[0] + d_skip[1]).reshape(g, r)[..., None]
    y = y_f + y_b + skip.reshape(b, s, SSM_INNER)
    gated = (y * jax.nn.silu(z.astype(jnp.float32))).reshape(b, s, g, SSM_INNER // g)
    out = rmsnorm(gated, norm_w.reshape(g, SSM_INNER // g))
    return out.reshape(b, s, SSM_INNER).astype(z.dtype)


def conv_glu_ffn(h, w_in, conv_w, conv_b, w_out):
    gu = h @ w_in
    gate, up = jnp.split(gu, 2, axis=-1)
    gate = dwconv_centred(gate, conv_w, conv_b)
    return (jax.nn.silu(gate) * up) @ w_out


def setup_inputs(seed: int = 0) -> dict:
    key = jax.random.key(seed)
    ks = jax.random.split(key, 25)
    f32 = jnp.float32
    L, D = DEPTH, D_MODEL

    def nrm(k, shape, scale):
        return jax.random.normal(k, shape, f32) * scale

    def gain(k, shape):
        return 1.0 + 0.02 * jax.random.normal(k, shape, f32)

    dt0 = jnp.exp(jax.random.uniform(ks[14], (L, 2, SSM_HEADS), f32, math.log(1e-3), math.log(1e-1)))
    dt0 = jnp.maximum(dt0, 1e-4)
    dt_bias = dt0 + jnp.log(-jnp.expm1(-dt0))
    a_log = jnp.log(jax.random.uniform(ks[15], (L, 2, SSM_HEADS), f32, 1.0, 16.0))
    positions = jnp.broadcast_to(jnp.arange(SEQ, dtype=jnp.int32)[None, :], (BATCH, SEQ))
    return {
        'x': nrm(ks[0], (BATCH, SEQ, D), 1.0),
        'c': nrm(ks[1], (BATCH, D), 1.0),
        'positions': positions,
        'w_ada': nrm(ks[2], (L, D, N_MOD * D), 0.5 * D ** -0.5),
        'b_ada': nrm(ks[3], (L, N_MOD * D), 0.02),
        'norm1_w': gain(ks[4], (L, D)),
        'w_in': nrm(ks[5], (L, D, IN_COLS), D ** -0.5),
        'lambda_q1': nrm(ks[6], (L, ATT_HEAD_DIM), 0.1),
        'lambda_k1': nrm(ks[7], (L, ATT_HEAD_DIM), 0.1),
        'lambda_q2': nrm(ks[8], (L, ATT_HEAD_DIM), 0.1),
        'lambda_k2': nrm(ks[9], (L, ATT_HEAD_DIM), 0.1),
        'subln_w': gain(ks[10], (L, 2 * ATT_HEAD_DIM)),
        'conv_w': nrm(ks[11], (L, SSM_CONV, CONV_CH), SSM_CONV ** -0.5),
        'conv_b': nrm(ks[12], (L, CONV_CH), 0.02),
        'dt_bias': dt_bias,
        'a_log': a_log,
        'd_skip': 1.0 + nrm(ks[13], (L, 2, SSM_HEADS), 0.1),
        'ssm_norm_w': gain(ks[16], (L, SSM_INNER)),
        'w_out': nrm(ks[17], (L, MIX_WIDTH, D), MIX_WIDTH ** -0.5),
        'norm2_w': gain(ks[18], (L, D)),
        'w_ffn_in': nrm(ks[19], (L, D, 2 * D_FF), D ** -0.5),
        'ffn_conv_w': nrm(ks[20], (L, FFN_CONV, D_FF), FFN_CONV ** -0.5),
        'ffn_conv_b': nrm(ks[21], (L, D_FF), 0.02),
        'w_ffn_out': nrm(ks[22], (L, D_FF, D), D_FF ** -0.5),
        'final_norm_w': gain(ks[23], (D,)),
    }


def reference(x, c, positions, w_ada, b_ada, norm1_w, w_in, lambda_q1, lambda_k1, lambda_q2,
              lambda_k2, subln_w, conv_w, conv_b, dt_bias, a_log, d_skip, ssm_norm_w, w_out,
              norm2_w, w_ffn_in, ffn_conv_w, ffn_conv_b, w_ffn_out, final_norm_w):
    b, s, _ = x.shape
    inv_freq = ROPE_THETA ** (-jnp.arange(0, ROPE_DIM, 2, dtype=jnp.float32) / ROPE_DIM)
    ang = positions.astype(jnp.float32)[..., None] * inv_freq
    cos = jnp.cos(ang)[:, :, None, :].astype(x.dtype)
    sin = jnp.sin(ang)[:, :, None, :].astype(x.dtype)
    c_act = jax.nn.silu(c)
    for l in range(DEPTH):
        lam_init = 0.8 - 0.6 * math.exp(-0.3 * l)
        mod = c_act @ w_ada[l] + b_ada[l]
        sh1, sc1, g1, sh2, sc2, g2 = [m[:, None, :] for m in jnp.split(mod, N_MOD, axis=-1)]

        h = rmsnorm(x, norm1_w[l]) * (1.0 + sc1) + sh1
        proj = h @ w_in[l]
        q, k, v, z, xbc, dt_raw = jnp.split(proj, IN_SPLITS, axis=-1)
        q = partial_rope(q.reshape(b, s, 2 * ATT_HEADS, ATT_HEAD_DIM), cos, sin)
        k = partial_rope(k.reshape(b, s, 2 * ATT_HEADS, ATT_HEAD_DIM), cos, sin)
        v = v.reshape(b, s, ATT_HEADS, 2 * ATT_HEAD_DIM)
        lam = (jnp.exp(jnp.sum(lambda_q1[l].astype(jnp.float32) * lambda_k1[l].astype(jnp.float32)))
               - jnp.exp(jnp.sum(lambda_q2[l].astype(jnp.float32) * lambda_k2[l].astype(jnp.float32)))
               + lam_init)
        att = diff_attention(q, k, v, lam, lam_init, subln_w[l])
        ssm = ssd_mixer(z, xbc, dt_raw, conv_w[l], conv_b[l], dt_bias[l], a_log[l],
                        d_skip[l], ssm_norm_w[l])
        mix = jnp.concatenate([att, ssm], axis=-1) @ w_out[l]
        x = x + g1 * mix

        h = rmsnorm(x, norm2_w[l]) * (1.0 + sc2) + sh2
        x = x + g2 * conv_glu_ffn(h, w_ffn_in[l], ffn_conv_w[l], ffn_conv_b[l], w_ffn_out[l])
    return rmsnorm(x, final_norm_w)
```

```python
import functools
import math

import jax
import jax.numpy as jnp
from jax import lax
from jax.experimental import pallas as pl
from jax.experimental.pallas import tpu as pltpu

F32 = jnp.float32
BF16 = jnp.bfloat16

D_MODEL = 1024
ATT_HEADS = 4
ATT_HEAD_DIM = 64
ATT_WIDTH = ATT_HEADS * 2 * ATT_HEAD_DIM
ROPE_DIM = ATT_HEAD_DIM // 4
ROPE_THETA = 500000.0
SSM_INNER = 512
SSM_HEAD_DIM = 64
SSM_HEADS = SSM_INNER // SSM_HEAD_DIM
SSM_GROUPS = 2
SSM_STATE = 128
SSM_CHUNK = 128
CONV_CH = SSM_INNER + 2 * SSM_GROUPS * SSM_STATE
D_FF = 2816
N_MOD = 6
EPS = 1e-6

LANES = 128
HALO = 16
VMEM_LIMIT = 56 * 1024 * 1024

IN_MAIN = 3 * ATT_WIDTH + SSM_INNER + CONV_CH
IN_PAD = IN_MAIN + LANES

TM_IN = 512
TQ = 256
TM_MLP = 512
FF_CHUNK = 256


def _silu(x):
    return x * jax.nn.sigmoid(x)


def _const_spec(shape):
    zeros = (0,) * len(shape)
    return pl.BlockSpec(shape, lambda *_: zeros)


def _ada_kernel(c_ref, w_ref, b_ref, o_ref):
    c = c_ref[...]
    ca = _silu(c).astype(BF16)
    o_ref[...] = jnp.dot(ca, w_ref[...].astype(BF16), preferred_element_type=F32) + b_ref[...]


def _ada(c_pad, w_ada, b_ada):
    rows = c_pad.shape[0]
    n = w_ada.shape[1]
    tn = D_MODEL
    return pl.pallas_call(
        _ada_kernel,
        grid=(n // tn,),
        in_specs=[pl.BlockSpec((rows, D_MODEL), lambda j: (0, 0)),
                  pl.BlockSpec((D_MODEL, tn), lambda j: (0, j)),
                  pl.BlockSpec((1, tn), lambda j: (0, j))],
        out_specs=pl.BlockSpec((rows, tn), lambda j: (0, j)),
        out_shape=jax.ShapeDtypeStruct((rows, n), F32),
        name="ada",
    )(c_pad, w_ada, b_ada)


def _inproj_kernel(x_ref, pos_ref, invf_ref, nw_ref, sc_ref, sh_ref, w_ref,
                   q_ref, k_ref, v_ref, z_ref, xbc_ref, dt_ref):
    x = x_ref[...]
    ms = jnp.mean(x * x, axis=-1, keepdims=True)
    h = x * lax.rsqrt(ms + EPS) * nw_ref[...]
    h = h * (1.0 + sc_ref[...]) + sh_ref[...]
    proj = jnp.dot(h.astype(BF16), w_ref[...], preferred_element_type=F32)

    ang = pos_ref[...] * invf_ref[...]
    cos = jnp.cos(ang)
    sin = jnp.sin(ang)
    lane = lax.broadcasted_iota(jnp.int32, (1, LANES), 1) & (ATT_HEAD_DIM - 1)
    half = ROPE_DIM // 2
    c_full = jnp.where(lane < ROPE_DIM, cos, 1.0)
    s_lo = jnp.where(lane < half, -sin, 0.0)
    s_hi = jnp.where((lane >= half) & (lane < ROPE_DIM), sin, 0.0)

    def rope(t):
        return (t * c_full + pltpu.roll(t, LANES - half, axis=1) * s_lo
                + pltpu.roll(t, half, axis=1) * s_hi)

    scale = ATT_HEAD_DIM ** -0.5
    for j in range(ATT_WIDTH // LANES):
        lo = j * LANES
        q_ref[:, lo:lo + LANES] = (rope(proj[:, lo:lo + LANES]) * scale).astype(BF16)
        k_ref[:, lo:lo + LANES] = rope(proj[:, ATT_WIDTH + lo:ATT_WIDTH + lo + LANES]).astype(BF16)
    v_ref[...] = proj[:, 2 * ATT_WIDTH:3 * ATT_WIDTH].astype(BF16)
    z_ref[...] = proj[:, 3 * ATT_WIDTH:3 * ATT_WIDTH + SSM_INNER]
    xbc_ref[...] = proj[:, 3 * ATT_WIDTH + SSM_INNER:IN_MAIN]
    dt_ref[...] = proj[:, IN_MAIN:IN_PAD]


def _inproj(x, pos, invf, nw, sc, sh, w):
    b, s, d = x.shape
    tm = TM_IN
    row = lambda width: pl.BlockSpec((None, tm, width), lambda bi, i: (bi, i, 0))
    mod = pl.BlockSpec((None, 1, d), lambda bi, i: (bi, 0, 0))
    shp = lambda width, dt: jax.ShapeDtypeStruct((b, s, width), dt)
    return pl.pallas_call(
        _inproj_kernel,
        grid=(b, s // tm),
        in_specs=[row(d), row(1), _const_spec((1, LANES)), _const_spec((1, d)), mod, mod,
                  _const_spec((d, IN_PAD))],
        out_specs=[row(ATT_WIDTH), row(ATT_WIDTH), row(ATT_WIDTH), row(SSM_INNER), row(CONV_CH),
                   row(LANES)],
        out_shape=[shp(ATT_WIDTH, BF16), shp(ATT_WIDTH, BF16), shp(ATT_WIDTH, BF16),
                   shp(SSM_INNER, F32), shp(CONV_CH, F32), shp(LANES, F32)],
        compiler_params=pltpu.CompilerParams(
            dimension_semantics=("parallel", "parallel"), vmem_limit_bytes=VMEM_LIMIT),
        name="inproj",
    )(x, pos, invf, nw, sc, sh, w)


def _attn_kernel(q_ref, k_ref, v_ref, lq1_ref, lk1_ref, lq2_ref, lk2_ref, sw_ref, o_ref,
                 *, lam_init):
    tq = q_ref.shape[0]
    lam = (jnp.exp(jnp.sum(lq1_ref[...] * lk1_ref[...], axis=-1, keepdims=True))
           - jnp.exp(jnp.sum(lq2_ref[...] * lk2_ref[...], axis=-1, keepdims=True)) + lam_init)
    q = q_ref[...]
    lane = lax.broadcasted_iota(jnp.int32, (1, LANES), 1)
    zero = jnp.zeros_like(q)
    qq = jnp.concatenate([jnp.where(lane < ATT_HEAD_DIM, q, zero),
                          jnp.where(lane >= ATT_HEAD_DIM, q, zero)], axis=0)
    s = lax.dot_general(qq, k_ref[...], (((1,), (1,)), ((), ())), preferred_element_type=F32)
    m = jnp.max(s, axis=-1, keepdims=True)
    p = jnp.exp(s - m)
    r = 1.0 / jnp.sum(p, axis=-1, keepdims=True)
    a = p[:tq] * r[:tq] - p[tq:] * (lam * r[tq:])
    o = jnp.dot(a.astype(BF16), v_ref[...], preferred_element_type=F32)
    ms = jnp.mean(o * o, axis=-1, keepdims=True)
    o = o * lax.rsqrt(ms + EPS) * sw_ref[...] * (1.0 - lam_init)
    o_ref[...] = o.astype(BF16)


def _attn(q, k, v, lq1, lk1, lq2, lk2, sw, lam_init):
    b, s, _ = q.shape
    qspec = pl.BlockSpec((None, TQ, LANES), lambda bi, h, i: (bi, i, h))
    kspec = pl.BlockSpec((None, s, LANES), lambda bi, h, i: (bi, 0, h))
    lspec = _const_spec((1, ATT_HEAD_DIM))
    return pl.pallas_call(
        functools.partial(_attn_kernel, lam_init=lam_init),
        grid=(b, ATT_HEADS, s // TQ),
        in_specs=[qspec, kspec, kspec, lspec, lspec, lspec, lspec, _const_spec((1, LANES))],
        out_specs=qspec,
        out_shape=jax.ShapeDtypeStruct((b, s, ATT_WIDTH), BF16),
        compiler_params=pltpu.CompilerParams(
            dimension_semantics=("parallel", "parallel", "parallel"),
            vmem_limit_bytes=VMEM_LIMIT),
        name="attn",
    )(q, k, v, lq1, lk1, lq2, lk2, sw)


def _ssd_kernel(xm_ref, xp_ref, xn_ref, dt_ref, z_ref, cw_ref, cb_ref, dtb_ref, alog_ref,
                dsk_ref, nw_ref, o_ref, h_sc, yf_sc, *, n_chunks):
    q = SSM_CHUNK
    t = pl.program_id(1)
    is_bwd = t >= n_chunks
    c = jnp.where(is_bwd, 2 * n_chunks - 1 - t, t)

    @pl.when((t == 0) | (t == n_chunks))
    def _():
        h_sc[...] = jnp.zeros_like(h_sc)

    xm = xm_ref[...]
    prev_row = jnp.where(c > 0, xp_ref[HALO - 1:HALO, :], 0.0)
    next_row = jnp.where(c < n_chunks - 1, xn_ref[0:1, :], 0.0)
    rows = lax.broadcasted_iota(jnp.int32, (q, 1), 0)
    x_up = jnp.where(rows == 0, prev_row, pltpu.roll(xm, 1, axis=0))
    x_dn = jnp.where(rows == q - 1, next_row, pltpu.roll(xm, q - 1, axis=0))
    xc = _silu(x_up * cw_ref[0:1, :] + xm * cw_ref[1:2, :] + x_dn * cw_ref[2:3, :] + cb_ref[...])
    xs = xc[:, :SSM_INNER]
    gn = SSM_GROUPS * SSM_STATE
    bm = xc[:, SSM_INNER:SSM_INNER + gn].astype(BF16)
    cm = xc[:, SSM_INNER + gn:].astype(BF16)

    dtb = jnp.where(is_bwd, dtb_ref[1:2, :], dtb_ref[0:1, :])
    alog = jnp.where(is_bwd, alog_ref[1:2, :], alog_ref[0:1, :])
    dt = jax.nn.softplus(dt_ref[...] + dtb)
    a = dt * (-jnp.exp(alog))
    ii = lax.broadcasted_iota(jnp.int32, (q, q), 0)
    jj = lax.broadcasted_iota(jnp.int32, (q, q), 1)
    keep = (jj - ii) * jnp.where(is_bwd, -1, 1) <= 0
    cs = jnp.dot(keep.astype(F32), a, precision=lax.Precision.HIGHEST,
                 preferred_element_type=F32)
    total = jnp.where(is_bwd, cs[0:1, :], cs[q - 1:q, :])
    cs_t = cs.T

    head_of_lane = lax.broadcasted_iota(jnp.int32, (LANES, SSM_INNER), 1) // SSM_HEAD_DIM
    expand = (lax.broadcasted_iota(jnp.int32, (LANES, SSM_INNER), 0) == head_of_lane).astype(F32)
    spread = lambda w: jnp.dot(w, expand, precision=lax.Precision.HIGHEST,
                               preferred_element_type=F32)
    dt_e = spread(dt)
    ecs_e = spread(jnp.exp(cs))
    dst_e = spread(dt * jnp.exp(total - cs))
    cdec_e = jnp.where(is_bwd, ecs_e[0:1, :], ecs_e[q - 1:q, :])

    xdt = xs * dt_e
    xst = (xs * dst_e).astype(BF16)
    h_prev = h_sc[...]
    lane = lax.broadcasted_iota(jnp.int32, (1, LANES), 1)
    y_parts = []
    new_states = []
    for g in range(SSM_GROUPS):
        b_g = bm[:, g * SSM_STATE:(g + 1) * SSM_STATE]
        c_g = cm[:, g * SSM_STATE:(g + 1) * SSM_STATE]
        cb = lax.dot_general(c_g, b_g, (((1,), (1,)), ((), ())), preferred_element_type=F32)
        width = SSM_INNER // SSM_GROUPS
        for blk in range(width // LANES):
            lo = g * width + blk * LANES
            x_blk = xdt[:, lo:lo + LANES]
            y_blk = jnp.zeros((q, LANES), F32)
            for sub in range(LANES // SSM_HEAD_DIM):
                head = lo // SSM_HEAD_DIM + sub
                diff = cs[:, head:head + 1] - cs_t[head:head + 1, :]
                decay = jnp.where(keep, jnp.exp(diff), 0.0)
                in_head = (lane // SSM_HEAD_DIM) == sub
                x_h = jnp.where(in_head, x_blk, 0.0).astype(BF16)
                y_blk = y_blk + jnp.dot((cb * decay).astype(BF16), x_h,
                                        preferred_element_type=F32)
            y_parts.append(y_blk)
        h_g = h_prev[:, g * width:(g + 1) * width]
        y_off = jnp.dot(c_g, h_g.astype(BF16), preferred_element_type=F32)
        y_parts[-2] = y_parts[-2] + y_off[:, :LANES] * ecs_e[:, g * width:g * width + LANES]
        y_parts[-1] = y_parts[-1] + y_off[:, LANES:] * ecs_e[:, g * width + LANES:(g + 1) * width]
        new_states.append(lax.dot_general(b_g, xst[:, g * width:(g + 1) * width],
                                          (((0,), (0,)), ((), ())), preferred_element_type=F32))
    h_sc[...] = cdec_e * h_prev + jnp.concatenate(new_states, axis=1)
    y = jnp.concatenate(y_parts, axis=1)
    row0 = pl.multiple_of(c * q, q)

    @pl.when(jnp.logical_not(is_bwd))
    def _():
        yf_sc[pl.ds(row0, q), :] = y

    @pl.when(is_bwd)
    def _():
        tot = yf_sc[pl.ds(row0, q), :] + y + xs * (dsk_ref[0:1, :] + dsk_ref[1:2, :])
        gated = tot * _silu(z_ref[...])
        width = SSM_INNER // SSM_GROUPS
        for g in range(SSM_GROUPS):
            gg = gated[:, g * width:(g + 1) * width]
            ms = jnp.mean(gg * gg, axis=-1, keepdims=True)
            o_ref[:, g * width:(g + 1) * width] = (
                gg * lax.rsqrt(ms + EPS) * nw_ref[:, g * width:(g + 1) * width]).astype(BF16)


def _ssd(xbc, dt, z, cw, cb, dtb, alog, dsk, nw):
    b, s, _ = xbc.shape
    nc = s // SSM_CHUNK
    hb = SSM_CHUNK // HALO
    n_halo = s // HALO

    def chunk(t):
        return jnp.where(t >= nc, 2 * nc - 1 - t, t)

    def late(t):
        return jnp.where(t >= nc, 2 * nc - 1 - t, nc - 1)

    in_specs = [
        pl.BlockSpec((None, SSM_CHUNK, CONV_CH), lambda bi, t: (bi, chunk(t), 0)),
        pl.BlockSpec((None, HALO, CONV_CH), lambda bi, t: (bi, jnp.maximum(chunk(t) * hb - 1, 0), 0)),
        pl.BlockSpec((None, HALO, CONV_CH),
                     lambda bi, t: (bi, jnp.minimum(chunk(t) * hb + hb, n_halo - 1), 0)),
        pl.BlockSpec((None, SSM_CHUNK, LANES), lambda bi, t: (bi, chunk(t), 0)),
        pl.BlockSpec((None, SSM_CHUNK, SSM_INNER), lambda bi, t: (bi, late(t), 0)),
        _const_spec((3, CONV_CH)), _const_spec((1, CONV_CH)), _const_spec((2, LANES)),
        _const_spec((2, LANES)), _const_spec((2, SSM_INNER)), _const_spec((1, SSM_INNER)),
    ]
    return pl.pallas_call(
        functools.partial(_ssd_kernel, n_chunks=nc),
        grid=(b, 2 * nc),
        in_specs=in_specs,
        out_specs=pl.BlockSpec((None, SSM_CHUNK, SSM_INNER), lambda bi, t: (bi, late(t), 0)),
        out_shape=jax.ShapeDtypeStruct((b, s, SSM_INNER), BF16),
        scratch_shapes=[pltpu.VMEM((SSM_STATE, SSM_INNER), F32), pltpu.VMEM((s, SSM_INNER), F32)],
        compiler_params=pltpu.CompilerParams(
            dimension_semantics=("parallel", "arbitrary"), vmem_limit_bytes=VMEM_LIMIT),
        name="ssd",
    )(xbc, xbc, xbc, dt, z, cw, cb, dtb, alog, dsk, nw)


def _mlp_kernel(x_ref, xp_ref, xn_ref, a_ref, ap_ref, an_ref, s_ref, sp_ref, sn_ref,
                g1_ref, sc_ref, sh_ref, g2_ref, woa_ref, wos_ref, n2_ref, wg_ref, wu_ref,
                cw_ref, cb_ref, wfo_ref, fw_ref, o_ref, *, n_tiles):
    tm = x_ref.shape[0]
    i = pl.program_id(1)
    cat = lambda p, m, n: jnp.concatenate([p[...], m[...], n[...]], axis=0)
    xe = cat(xp_ref, x_ref, xn_ref)
    mix = (jnp.dot(cat(ap_ref, a_ref, an_ref), woa_ref[...], preferred_element_type=F32)
           + jnp.dot(cat(sp_ref, s_ref, sn_ref), wos_ref[...], preferred_element_type=F32))
    x1 = xe + g1_ref[...] * mix
    ms = jnp.mean(x1 * x1, axis=-1, keepdims=True)
    h = x1 * lax.rsqrt(ms + EPS) * n2_ref[...]
    h = (h * (1.0 + sc_ref[...]) + sh_ref[...]).astype(BF16)
    hm = h[HALO:HALO + tm]

    ext = tm + 2 * HALO
    rows = lax.broadcasted_iota(jnp.int32, (ext, 1), 0)
    valid = ((rows >= HALO) | (i > 0)) & ((rows < HALO + tm) | (i < n_tiles - 1))
    acc = jnp.zeros((tm, D_MODEL), F32)
    for j in range(D_FF // FF_CHUNK):
        lo = j * FF_CHUNK
        gate = jnp.dot(h, wg_ref[:, lo:lo + FF_CHUNK], preferred_element_type=F32)
        gate = jnp.where(valid, gate, 0.0)
        g_up = pltpu.roll(gate, 1, axis=0)[HALO:HALO + tm]
        g_dn = pltpu.roll(gate, ext - 1, axis=0)[HALO:HALO + tm]
        conv = (g_up * cw_ref[0:1, lo:lo + FF_CHUNK] + gate[HALO:HALO + tm] * cw_ref[1:2, lo:lo + FF_CHUNK]
                + g_dn * cw_ref[2:3, lo:lo + FF_CHUNK] + cb_ref[:, lo:lo + FF_CHUNK])
        up = jnp.dot(hm, wu_ref[:, lo:lo + FF_CHUNK], preferred_element_type=F32)
        act = (_silu(conv) * up).astype(BF16)
        acc = acc + jnp.dot(act, wfo_ref[lo:lo + FF_CHUNK, :], preferred_element_type=F32)
    x2 = x1[HALO:HALO + tm] + g2_ref[...] * acc
    ms2 = jnp.mean(x2 * x2, axis=-1, keepdims=True)
    o_ref[...] = x2 * lax.rsqrt(ms2 + EPS) * fw_ref[...]


def _mlp(x, att, ssm, g1, sc2, sh2, g2, woa, wos, n2, wg, wu, cw, cb, wfo, fw):
    b, s, d = x.shape
    tm = TM_MLP
    nt = s // tm
    hb = tm // HALO
    n_halo = s // HALO

    def trio(width):
        return [pl.BlockSpec((None, tm, width), lambda bi, i: (bi, i, 0)),
                pl.BlockSpec((None, HALO, width), lambda bi, i: (bi, jnp.maximum(i * hb - 1, 0), 0)),
                pl.BlockSpec((None, HALO, width),
                             lambda bi, i: (bi, jnp.minimum(i * hb + hb, n_halo - 1), 0))]

    mod = pl.BlockSpec((None, 1, d), lambda bi, i: (bi, 0, 0))
    in_specs = (trio(d) + trio(ATT_WIDTH) + trio(SSM_INNER) + [mod, mod, mod, mod]
                + [_const_spec((ATT_WIDTH, d)), _const_spec((SSM_INNER, d)), _const_spec((1, d)),
                   _const_spec((d, D_FF)), _const_spec((d, D_FF)), _const_spec((3, D_FF)),
                   _const_spec((1, D_FF)), _const_spec((D_FF, d)), _const_spec((1, d))])
    return pl.pallas_call(
        functools.partial(_mlp_kernel, n_tiles=nt),
        grid=(b, nt),
        in_specs=in_specs,
        out_specs=pl.BlockSpec((None, tm, d), lambda bi, i: (bi, i, 0)),
        out_shape=jax.ShapeDtypeStruct((b, s, d), F32),
        compiler_params=pltpu.CompilerParams(
            dimension_semantics=("parallel", "parallel"), vmem_limit_bytes=VMEM_LIMIT),
        name="mlp",
    )(x, x, x, att, att, att, ssm, ssm, ssm, g1, sc2, sh2, g2, woa, wos, n2, wg, wu, cw, cb,
      wfo, fw)


def kernel(x, c, positions, w_ada, b_ada, norm1_w, w_in, lambda_q1, lambda_k1, lambda_q2,
           lambda_k2, subln_w, conv_w, conv_b, dt_bias, a_log, d_skip, ssm_norm_w, w_out,
           norm2_w, w_ffn_in, ffn_conv_w, ffn_conv_b, w_ffn_out, final_norm_w):
    b, s, d = x.shape
    depth = w_ada.shape[0]
    assert depth == 1, "the fused mlp kernel applies the final norm, so exactly one layer"
    inv_freq = ROPE_THETA ** (-jnp.arange(0, ROPE_DIM, 2, dtype=F32) / ROPE_DIM)
    invf = jnp.tile(inv_freq, LANES // inv_freq.shape[0])[None, :]
    pos = positions.astype(F32)[..., None]
    c_pad = jnp.pad(c, ((0, 8 - b % 8 if b % 8 else 0), (0, 0)))
    pad_heads = lambda p: jnp.pad(p, ((0, 0), (0, LANES - SSM_HEADS)))
    for l in range(depth):
        lam_init = 0.8 - 0.6 * math.exp(-0.3 * l)
        mod = _ada(c_pad, w_ada[l], b_ada[l][None, :])[:b]
        sh1, sc1, g1, sh2, sc2, g2 = [m[:, None, :] for m in jnp.split(mod, N_MOD, axis=-1)]

        w_l = w_in[l]
        w_cat = jnp.concatenate(
            [w_l[:, :IN_MAIN], jnp.pad(w_l[:, IN_MAIN:], ((0, 0), (0, LANES - SSM_HEADS)))],
            axis=1).astype(BF16)
        q, k, v, z, xbc, dt = _inproj(x, pos, invf, norm1_w[l][None, :], sc1, sh1, w_cat)
        att = _attn(q, k, v, lambda_q1[l][None, :], lambda_k1[l][None, :], lambda_q2[l][None, :],
                    lambda_k2[l][None, :], subln_w[l][None, :], lam_init)
        ssm = _ssd(xbc, dt, z, conv_w[l], conv_b[l][None, :], pad_heads(dt_bias[l]),
                   pad_heads(a_log[l]), jnp.repeat(d_skip[l], SSM_HEAD_DIM, axis=-1),
                   ssm_norm_w[l][None, :])
        w_o = w_out[l].astype(BF16)
        w_fi = w_ffn_in[l].astype(BF16)
        x = _mlp(x, att, ssm, g1, sc2, sh2, g2, w_o[:ATT_WIDTH], w_o[ATT_WIDTH:],
                 norm2_w[l][None, :], w_fi[:, :D_FF], w_fi[:, D_FF:], ffn_conv_w[l],
                 ffn_conv_b[l][None, :], w_ffn_out[l].astype(BF16), final_norm_w[None, :])
    return x
```

```python
import functools
import math

import jax
import jax.numpy as jnp
from jax import lax
from jax.experimental import pallas as pl
from jax.experimental.pallas import tpu as pltpu

F32 = jnp.float32
BF16 = jnp.bfloat16

D_MODEL = 1024
ATT_HEADS = 4
ATT_HEAD_DIM = 64
ATT_WIDTH = ATT_HEADS * 2 * ATT_HEAD_DIM
ROPE_DIM = ATT_HEAD_DIM // 4
ROPE_THETA = 500000.0
SSM_INNER = 512
SSM_HEAD_DIM = 64
SSM_HEADS = SSM_INNER // SSM_HEAD_DIM
SSM_GROUPS = 2
SSM_STATE = 128
SSM_CHUNK = 128
CONV_CH = SSM_INNER + 2 * SSM_GROUPS * SSM_STATE
D_FF = 2816
N_MOD = 6
EPS = 1e-6

LANES = 128
HALO = 16
VMEM_LIMIT = 56 * 1024 * 1024

IN_MAIN = 3 * ATT_WIDTH + SSM_INNER + CONV_CH
IN_PAD = IN_MAIN + LANES

TM_IN = 512
TQ = 256
ATT_CK = 512
ATT_RG = 64
TM_MLP = 512
FF_CHUNK = 256


def _silu(x):
    return x * jax.nn.sigmoid(x)


def _const_spec(shape):
    zeros = (0,) * len(shape)
    return pl.BlockSpec(shape, lambda *_: zeros)


def _ada_kernel(c_ref, w_ref, b_ref, o_ref):
    c = c_ref[...]
    ca = _silu(c).astype(BF16)
    o_ref[...] = jnp.dot(ca, w_ref[...].astype(BF16), preferred_element_type=F32) + b_ref[...]


def _ada(c_pad, w_ada, b_ada):
    rows = c_pad.shape[0]
    n = w_ada.shape[1]
    tn = D_MODEL
    return pl.pallas_call(
        _ada_kernel,
        grid=(n // tn,),
        in_specs=[pl.BlockSpec((rows, D_MODEL), lambda j: (0, 0)),
                  pl.BlockSpec((D_MODEL, tn), lambda j: (0, j)),
                  pl.BlockSpec((1, tn), lambda j: (0, j))],
        out_specs=pl.BlockSpec((rows, tn), lambda j: (0, j)),
        out_shape=jax.ShapeDtypeStruct((rows, n), F32),
        name="ada",
    )(c_pad, w_ada, b_ada)


def _inproj_kernel(x_ref, pos_ref, invf_ref, nw_ref, sc_ref, sh_ref, w_ref,
                   q_ref, k_ref, v_ref, z_ref, xbc_ref, dt_ref):
    x = x_ref[...]
    ms = jnp.mean(x * x, axis=-1, keepdims=True)
    h = x * lax.rsqrt(ms + EPS) * nw_ref[...]
    h = h * (1.0 + sc_ref[...]) + sh_ref[...]
    proj = jnp.dot(h.astype(BF16), w_ref[...], preferred_element_type=F32)

    ang = pos_ref[...] * invf_ref[...]
    cos = jnp.cos(ang)
    sin = jnp.sin(ang)
    lane = lax.broadcasted_iota(jnp.int32, (1, LANES), 1) & (ATT_HEAD_DIM - 1)
    half = ROPE_DIM // 2
    c_full = jnp.where(lane < ROPE_DIM, cos, 1.0)
    s_lo = jnp.where(lane < half, -sin, 0.0)
    s_hi = jnp.where((lane >= half) & (lane < ROPE_DIM), sin, 0.0)

    def rope(t):
        return (t * c_full + pltpu.roll(t, LANES - half, axis=1) * s_lo
                + pltpu.roll(t, half, axis=1) * s_hi)

    scale = ATT_HEAD_DIM ** -0.5
    for j in range(ATT_WIDTH // LANES):
        lo = j * LANES
        q_ref[:, lo:lo + LANES] = (rope(proj[:, lo:lo + LANES]) * scale).astype(BF16)
        k_ref[:, lo:lo + LANES] = rope(proj[:, ATT_WIDTH + lo:ATT_WIDTH + lo + LANES]).astype(BF16)
    v_ref[...] = proj[:, 2 * ATT_WIDTH:3 * ATT_WIDTH].astype(BF16)
    z_ref[...] = proj[:, 3 * ATT_WIDTH:3 * ATT_WIDTH + SSM_INNER]
    xbc_ref[...] = proj[:, 3 * ATT_WIDTH + SSM_INNER:IN_MAIN]
    dt_ref[...] = proj[:, IN_MAIN:IN_PAD]


def _inproj(x, pos, invf, nw, sc, sh, w):
    b, s, d = x.shape
    tm = TM_IN
    row = lambda width: pl.BlockSpec((None, tm, width), lambda bi, i: (bi, i, 0))
    mod = pl.BlockSpec((None, 1, d), lambda bi, i: (bi, 0, 0))
    shp = lambda width, dt: jax.ShapeDtypeStruct((b, s, width), dt)
    return pl.pallas_call(
        _inproj_kernel,
        grid=(b, s // tm),
        in_specs=[row(d), row(1), _const_spec((1, LANES)), _const_spec((1, d)), mod, mod,
                  _const_spec((d, IN_PAD))],
        out_specs=[row(ATT_WIDTH), row(ATT_WIDTH), row(ATT_WIDTH), row(SSM_INNER), row(CONV_CH),
                   row(LANES)],
        out_shape=[shp(ATT_WIDTH, BF16), shp(ATT_WIDTH, BF16), shp(ATT_WIDTH, BF16),
                   shp(SSM_INNER, F32), shp(CONV_CH, F32), shp(LANES, F32)],
        compiler_params=pltpu.CompilerParams(
            dimension_semantics=("parallel", "parallel"), vmem_limit_bytes=VMEM_LIMIT),
        name="inproj",
    )(x, pos, invf, nw, sc, sh, w)


def _attn_kernel(q_ref, k_ref, v_ref, lq1_ref, lk1_ref, lq2_ref, lk2_ref, sw_ref, o_ref,
                 s_a, s_b, m_a, m_b, mv_scr, l_scr, p_scr, *, lam_init):
    seq = k_ref.shape[0]
    n_blocks = seq // TQ
    n_chunks = seq // ATT_CK
    rows = 2 * TQ
    lam = (jnp.exp(jnp.sum(lq1_ref[...] * lk1_ref[...], axis=-1, keepdims=True))
           - jnp.exp(jnp.sum(lq2_ref[...] * lk2_ref[...], axis=-1, keepdims=True)) + lam_init)
    lane = lax.broadcasted_iota(jnp.int32, (1, LANES), 1)
    out_gain = sw_ref[...] * (1.0 - lam_init)

    def stacked_q(i):
        q = q_ref[pl.ds(pl.multiple_of(i * TQ, TQ), TQ), :]
        zero = jnp.zeros_like(q)
        return jnp.concatenate([jnp.where(lane < ATT_HEAD_DIM, q, zero),
                                jnp.where(lane >= ATT_HEAD_DIM, q, zero)], axis=0)

    groups = [slice(g * ATT_RG, (g + 1) * ATT_RG) for g in range(rows // ATT_RG)]
    tiles = lambda c: [slice(c * ATT_CK + j * LANES, c * ATT_CK + (j + 1) * LANES)
                       for j in range(ATT_CK // LANES)]

    def score_chunk(qq, c, s_scr):
        ks = slice(c * ATT_CK, (c + 1) * ATT_CK)
        s_scr[:, ks] = lax.dot_general(qq, k_ref[ks, :], (((1,), (1,)), ((), ())),
                                       preferred_element_type=F32)
        for rs in groups:
            mx = mv_scr[rs, :]
            for ts in tiles(c):
                mx = jnp.maximum(mx, s_scr[rs, ts])
            mv_scr[rs, :] = mx

    def finish_max(m_scr):
        m_scr[...] = jnp.broadcast_to(jnp.max(mv_scr[...], axis=-1, keepdims=True), (rows, LANES))
        mv_scr[...] = jnp.full((rows, LANES), -jnp.inf, F32)

    def value_chunk(c, s_scr, m_scr, acc):
        for rs in groups:
            mb = m_scr[rs, :]
            lsum = l_scr[rs, :]
            for ts in tiles(c):
                p = jnp.exp(s_scr[rs, ts] - mb)
                lsum = lsum + p
                p_scr[rs, ts] = p.astype(BF16)
            l_scr[rs, :] = lsum
        ks = slice(c * ATT_CK, (c + 1) * ATT_CK)
        return acc + jnp.dot(p_scr[:, ks], v_ref[ks, :], preferred_element_type=F32)

    def finish(i, acc):
        o = acc / jnp.sum(l_scr[...], axis=-1, keepdims=True)
        l_scr[...] = jnp.zeros((rows, LANES), F32)
        o = o[:TQ] - lam * o[TQ:]
        ms = jnp.mean(o * o, axis=-1, keepdims=True)
        o_ref[pl.ds(pl.multiple_of(i * TQ, TQ), TQ), :] = (
            o * lax.rsqrt(ms + EPS) * out_gain).astype(BF16)

    def overlapped(i_cur, s_cur, m_cur, i_next, s_next, m_next):
        qq = stacked_q(i_next)
        acc = jnp.zeros((rows, LANES), F32)
        for c in range(n_chunks):
            score_chunk(qq, c, s_next)
            acc = value_chunk(c, s_cur, m_cur, acc)
        finish(i_cur, acc)
        finish_max(m_next)

    mv_scr[...] = jnp.full((rows, LANES), -jnp.inf, F32)
    l_scr[...] = jnp.zeros((rows, LANES), F32)
    qq0 = stacked_q(0)
    for c in range(n_chunks):
        score_chunk(qq0, c, s_a)
    finish_max(m_a)

    @pl.loop(0, n_blocks // 2)
    def _(j):
        overlapped(2 * j, s_a, m_a, 2 * j + 1, s_b, m_b)
        overlapped(2 * j + 1, s_b, m_b, jnp.minimum(2 * j + 2, n_blocks - 1), s_a, m_a)


def _attn(q, k, v, lq1, lk1, lq2, lk2, sw, lam_init):
    b, s, _ = q.shape
    spec = pl.BlockSpec((None, s, LANES), lambda bi, h: (bi, 0, h))
    lspec = _const_spec((1, ATT_HEAD_DIM))
    return pl.pallas_call(
        functools.partial(_attn_kernel, lam_init=lam_init),
        grid=(b, ATT_HEADS),
        in_specs=[spec, spec, spec, lspec, lspec, lspec, lspec, _const_spec((1, LANES))],
        out_specs=spec,
        out_shape=jax.ShapeDtypeStruct((b, s, ATT_WIDTH), BF16),
        scratch_shapes=[pltpu.VMEM((2 * TQ, s + LANES), F32), pltpu.VMEM((2 * TQ, s + LANES), F32)]
        + [pltpu.VMEM((2 * TQ, LANES), F32)] * 4 + [pltpu.VMEM((2 * TQ, s + LANES), BF16)],
        compiler_params=pltpu.CompilerParams(
            dimension_semantics=("parallel", "parallel"), vmem_limit_bytes=VMEM_LIMIT),
        name="attn",
    )(q, k, v, lq1, lk1, lq2, lk2, sw)


def _ssd_kernel(xm_ref, xp_ref, xn_ref, dt_ref, z_ref, cw_ref, cb_ref, dtb_ref, alog_ref,
                dsk_ref, nw_ref, o_ref, h_sc, yf_sc, *, n_chunks):
    q = SSM_CHUNK
    t = pl.program_id(1)
    is_bwd = t >= n_chunks
    c = jnp.where(is_bwd, 2 * n_chunks - 1 - t, t)

    @pl.when((t == 0) | (t == n_chunks))
    def _():
        h_sc[...] = jnp.zeros_like(h_sc)

    xm = xm_ref[...]
    prev_row = jnp.where(c > 0, xp_ref[HALO - 1:HALO, :], 0.0)
    next_row = jnp.where(c < n_chunks - 1, xn_ref[0:1, :], 0.0)
    rows = lax.broadcasted_iota(jnp.int32, (q, 1), 0)
    x_up = jnp.where(rows == 0, prev_row, pltpu.roll(xm, 1, axis=0))
    x_dn = jnp.where(rows == q - 1, next_row, pltpu.roll(xm, q - 1, axis=0))
    xc = _silu(x_up * cw_ref[0:1, :] + xm * cw_ref[1:2, :] + x_dn * cw_ref[2:3, :] + cb_ref[...])
    xs = xc[:, :SSM_INNER]
    gn = SSM_GROUPS * SSM_STATE
    bm = xc[:, SSM_INNER:SSM_INNER + gn].astype(BF16)
    cm = xc[:, SSM_INNER + gn:].astype(BF16)

    dtb = jnp.where(is_bwd, dtb_ref[1:2, :], dtb_ref[0:1, :])
    alog = jnp.where(is_bwd, alog_ref[1:2, :], alog_ref[0:1, :])
    dt = jax.nn.softplus(dt_ref[...] + dtb)
    a = dt * (-jnp.exp(alog))
    ii = lax.broadcasted_iota(jnp.int32, (q, q), 0)
    jj = lax.broadcasted_iota(jnp.int32, (q, q), 1)
    keep = (jj - ii) * jnp.where(is_bwd, -1, 1) <= 0
    cs = jnp.dot(keep.astype(F32), a, precision=lax.Precision.HIGHEST,
                 preferred_element_type=F32)
    total = jnp.where(is_bwd, cs[0:1, :], cs[q - 1:q, :])
    cs_t = cs.T

    head_of_lane = lax.broadcasted_iota(jnp.int32, (LANES, SSM_INNER), 1) // SSM_HEAD_DIM
    expand = (lax.broadcasted_iota(jnp.int32, (LANES, SSM_INNER), 0) == head_of_lane).astype(F32)
    spread = lambda w: jnp.dot(w, expand, precision=lax.Precision.HIGHEST,
                               preferred_element_type=F32)
    dt_e = spread(dt)
    ecs_e = spread(jnp.exp(cs))
    dst_e = spread(dt * jnp.exp(total - cs))
    cdec_e = jnp.where(is_bwd, ecs_e[0:1, :], ecs_e[q - 1:q, :])

    xdt = xs * dt_e
    xst = (xs * dst_e).astype(BF16)
    h_prev = h_sc[...]
    lane = lax.broadcasted_iota(jnp.int32, (1, LANES), 1)
    y_parts = []
    new_states = []
    for g in range(SSM_GROUPS):
        b_g = bm[:, g * SSM_STATE:(g + 1) * SSM_STATE]
        c_g = cm[:, g * SSM_STATE:(g + 1) * SSM_STATE]
        cb = lax.dot_general(c_g, b_g, (((1,), (1,)), ((), ())), preferred_element_type=F32)
        width = SSM_INNER // SSM_GROUPS
        for blk in range(width // LANES):
            lo = g * width + blk * LANES
            x_blk = xdt[:, lo:lo + LANES]
            y_blk = jnp.zeros((q, LANES), F32)
            for sub in range(LANES // SSM_HEAD_DIM):
                head = lo // SSM_HEAD_DIM + sub
                diff = cs[:, head:head + 1] - cs_t[head:head + 1, :]
                decay = jnp.where(keep, jnp.exp(diff), 0.0)
                in_head = (lane // SSM_HEAD_DIM) == sub
                x_h = jnp.where(in_head, x_blk, 0.0).astype(BF16)
                y_blk = y_blk + jnp.dot((cb * decay).astype(BF16), x_h,
                                        preferred_element_type=F32)
            y_parts.append(y_blk)
        h_g = h_prev[:, g * width:(g + 1) * width]
        y_off = jnp.dot(c_g, h_g.astype(BF16), preferred_element_type=F32)
        y_parts[-2] = y_parts[-2] + y_off[:, :LANES] * ecs_e[:, g * width:g * width + LANES]
        y_parts[-1] = y_parts[-1] + y_off[:, LANES:] * ecs_e[:, g * width + LANES:(g + 1) * width]
        new_states.append(lax.dot_general(b_g, xst[:, g * width:(g + 1) * width],
                                          (((0,), (0,)), ((), ())), preferred_element_type=F32))
    h_sc[...] = cdec_e * h_prev + jnp.concatenate(new_states, axis=1)
    y = jnp.concatenate(y_parts, axis=1)
    row0 = pl.multiple_of(c * q, q)

    @pl.when(jnp.logical_not(is_bwd))
    def _():
        yf_sc[pl.ds(row0, q), :] = y

    @pl.when(is_bwd)
    def _():
        tot = yf_sc[pl.ds(row0, q), :] + y + xs * (dsk_ref[0:1, :] + dsk_ref[1:2, :])
        gated = tot * _silu(z_ref[...])
        width = SSM_INNER // SSM_GROUPS
        for g in range(SSM_GROUPS):
            gg = gated[:, g * width:(g + 1) * width]
            ms = jnp.mean(gg * gg, axis=-1, keepdims=True)
            o_ref[:, g * width:(g + 1) * width] = (
                gg * lax.rsqrt(ms + EPS) * nw_ref[:, g * width:(g + 1) * width]).astype(BF16)


def _ssd(xbc, dt, z, cw, cb, dtb, alog, dsk, nw):
    b, s, _ = xbc.shape
    nc = s // SSM_CHUNK
    hb = SSM_CHUNK // HALO
    n_halo = s // HALO

    def chunk(t):
        return jnp.where(t >= nc, 2 * nc - 1 - t, t)

    def late(t):
        return jnp.where(t >= nc, 2 * nc - 1 - t, nc - 1)

    in_specs = [
        pl.BlockSpec((None, SSM_CHUNK, CONV_CH), lambda bi, t: (bi, chunk(t), 0)),
        pl.BlockSpec((None, HALO, CONV_CH), lambda bi, t: (bi, jnp.maximum(chunk(t) * hb - 1, 0), 0)),
        pl.BlockSpec((None, HALO, CONV_CH),
                     lambda bi, t: (bi, jnp.minimum(chunk(t) * hb + hb, n_halo - 1), 0)),
        pl.BlockSpec((None, SSM_CHUNK, LANES), lambda bi, t: (bi, chunk(t), 0)),
        pl.BlockSpec((None, SSM_CHUNK, SSM_INNER), lambda bi, t: (bi, late(t), 0)),
        _const_spec((3, CONV_CH)), _const_spec((1, CONV_CH)), _const_spec((2, LANES)),
        _const_spec((2, LANES)), _const_spec((2, SSM_INNER)), _const_spec((1, SSM_INNER)),
    ]
    return pl.pallas_call(
        functools.partial(_ssd_kernel, n_chunks=nc),
        grid=(b, 2 * nc),
        in_specs=in_specs,
        out_specs=pl.BlockSpec((None, SSM_CHUNK, SSM_INNER), lambda bi, t: (bi, late(t), 0)),
        out_shape=jax.ShapeDtypeStruct((b, s, SSM_INNER), BF16),
        scratch_shapes=[pltpu.VMEM((SSM_STATE, SSM_INNER), F32), pltpu.VMEM((s, SSM_INNER), F32)],
        compiler_params=pltpu.CompilerParams(
            dimension_semantics=("parallel", "arbitrary"), vmem_limit_bytes=VMEM_LIMIT),
        name="ssd",
    )(xbc, xbc, xbc, dt, z, cw, cb, dtb, alog, dsk, nw)


def _mlp_kernel(x_ref, xp_ref, xn_ref, a_ref, ap_ref, an_ref, s_ref, sp_ref, sn_ref,
                g1_ref, sc_ref, sh_ref, g2_ref, woa_ref, wos_ref, n2_ref, wg_ref, wu_ref,
                cw_ref, cb_ref, wfo_ref, fw_ref, o_ref, *, n_tiles):
    tm = x_ref.shape[0]
    i = pl.program_id(1)
    cat = lambda p, m, n: jnp.concatenate([p[...], m[...], n[...]], axis=0)
    xe = cat(xp_ref, x_ref, xn_ref)
    mix = (jnp.dot(cat(ap_ref, a_ref, an_ref), woa_ref[...], preferred_element_type=F32)
           + jnp.dot(cat(sp_ref, s_ref, sn_ref), wos_ref[...], preferred_element_type=F32))
    x1 = xe + g1_ref[...] * mix
    ms = jnp.mean(x1 * x1, axis=-1, keepdims=True)
    h = x1 * lax.rsqrt(ms + EPS) * n2_ref[...]
    h = (h * (1.0 + sc_ref[...]) + sh_ref[...]).astype(BF16)
    hm = h[HALO:HALO + tm]

    ext = tm + 2 * HALO
    rows = lax.broadcasted_iota(jnp.int32, (ext, 1), 0)
    valid = ((rows >= HALO) | (i > 0)) & ((rows < HALO + tm) | (i < n_tiles - 1))
    acc = jnp.zeros((tm, D_MODEL), F32)
    for j in range(D_FF // FF_CHUNK):
        lo = j * FF_CHUNK
        gate = jnp.dot(h, wg_ref[:, lo:lo + FF_CHUNK], preferred_element_type=F32)
        gate = jnp.where(valid, gate, 0.0)
        g_up = pltpu.roll(gate, 1, axis=0)[HALO:HALO + tm]
        g_dn = pltpu.roll(gate, ext - 1, axis=0)[HALO:HALO + tm]
        conv = (g_up * cw_ref[0:1, lo:lo + FF_CHUNK] + gate[HALO:HALO + tm] * cw_ref[1:2, lo:lo + FF_CHUNK]
                + g_dn * cw_ref[2:3, lo:lo + FF_CHUNK] + cb_ref[:, lo:lo + FF_CHUNK])
        up = jnp.dot(hm, wu_ref[:, lo:lo + FF_CHUNK], preferred_element_type=F32)
        act = (_silu(conv) * up).astype(BF16)
        acc = acc + jnp.dot(act, wfo_ref[lo:lo + FF_CHUNK, :], preferred_element_type=F32)
    x2 = x1[HALO:HALO + tm] + g2_ref[...] * acc
    ms2 = jnp.mean(x2 * x2, axis=-1, keepdims=True)
    o_ref[...] = x2 * lax.rsqrt(ms2 + EPS) * fw_ref[...]


def _mlp(x, att, ssm, g1, sc2, sh2, g2, woa, wos, n2, wg, wu, cw, cb, wfo, fw):
    b, s, d = x.shape
    tm = TM_MLP
    nt = s // tm
    hb = tm // HALO
    n_halo = s // HALO

    def trio(width):
        return [pl.BlockSpec((None, tm, width), lambda bi, i: (bi, i, 0)),
                pl.BlockSpec((None, HALO, width), lambda bi, i: (bi, jnp.maximum(i * hb - 1, 0), 0)),
                pl.BlockSpec((None, HALO, width),
                             lambda bi, i: (bi, jnp.minimum(i * hb + hb, n_halo - 1), 0))]

    mod = pl.BlockSpec((None, 1, d), lambda bi, i: (bi, 0, 0))
    in_specs = (trio(d) + trio(ATT_WIDTH) + trio(SSM_INNER) + [mod, mod, mod, mod]
                + [_const_spec((ATT_WIDTH, d)), _const_spec((SSM_INNER, d)), _const_spec((1, d)),
                   _const_spec((d, D_FF)), _const_spec((d, D_FF)), _const_spec((3, D_FF)),
                   _const_spec((1, D_FF)), _const_spec((D_FF, d)), _const_spec((1, d))])
    return pl.pallas_call(
        functools.partial(_mlp_kernel, n_tiles=nt),
        grid=(b, nt),
        in_specs=in_specs,
        out_specs=pl.BlockSpec((None, tm, d), lambda bi, i: (bi, i, 0)),
        out_shape=jax.ShapeDtypeStruct((b, s, d), F32),
        compiler_params=pltpu.CompilerParams(
            dimension_semantics=("parallel", "parallel"), vmem_limit_bytes=VMEM_LIMIT),
        name="mlp",
    )(x, x, x, att, att, att, ssm, ssm, ssm, g1, sc2, sh2, g2, woa, wos, n2, wg, wu, cw, cb,
      wfo, fw)


def kernel(x, c, positions, w_ada, b_ada, norm1_w, w_in, lambda_q1, lambda_k1, lambda_q2,
           lambda_k2, subln_w, conv_w, conv_b, dt_bias, a_log, d_skip, ssm_norm_w, w_out,
           norm2_w, w_ffn_in, ffn_conv_w, ffn_conv_b, w_ffn_out, final_norm_w):
    b, s, d = x.shape
    depth = w_ada.shape[0]
    assert depth == 1, "the fused mlp kernel applies the final norm, so exactly one layer"
    inv_freq = ROPE_THETA ** (-jnp.arange(0, ROPE_DIM, 2, dtype=F32) / ROPE_DIM)
    invf = jnp.tile(inv_freq, LANES // inv_freq.shape[0])[None, :]
    pos = positions.astype(F32)[..., None]
    c_pad = jnp.pad(c, ((0, 8 - b % 8 if b % 8 else 0), (0, 0)))
    pad_heads = lambda p: jnp.pad(p, ((0, 0), (0, LANES - SSM_HEADS)))
    for l in range(depth):
        lam_init = 0.8 - 0.6 * math.exp(-0.3 * l)
        mod = _ada(c_pad, w_ada[l], b_ada[l][None, :])[:b]
        sh1, sc1, g1, sh2, sc2, g2 = [m[:, None, :] for m in jnp.split(mod, N_MOD, axis=-1)]

        w_l = w_in[l]
        w_cat = jnp.concatenate(
            [w_l[:, :IN_MAIN], jnp.pad(w_l[:, IN_MAIN:], ((0, 0), (0, LANES - SSM_HEADS)))],
            axis=1).astype(BF16)
        q, k, v, z, xbc, dt = _inproj(x, pos, invf, norm1_w[l][None, :], sc1, sh1, w_cat)
        att = _attn(q, k, v, lambda_q1[l][None, :], lambda_k1[l][None, :], lambda_q2[l][None, :],
                    lambda_k2[l][None, :], subln_w[l][None, :], lam_init)
        ssm = _ssd(xbc, dt, z, conv_w[l], conv_b[l][None, :], pad_heads(dt_bias[l]),
                   pad_heads(a_log[l]), jnp.repeat(d_skip[l], SSM_HEAD_DIM, axis=-1),
                   ssm_norm_w[l][None, :])
        w_o = w_out[l].astype(BF16)
        w_fi = w_ffn_in[l].astype(BF16)
        x = _mlp(x, att, ssm, g1, sc2, sh2, g2, w_o[:ATT_WIDTH], w_o[ATT_WIDTH:],
                 norm2_w[l][None, :], w_fi[:, :D_FF], w_fi[:, D_FF:], ffn_conv_w[l],
                 ffn_conv_b[l][None, :], w_ffn_out[l].astype(BF16), final_norm_w[None, :])
    return x
```
